```python
import math
import jax
import jax.numpy as jnp
from jax import lax
import numpy as np

D_MODEL = 1024
BATCH = 16
SEQ = 4096
DEPTH = 1
DEC_BATCH = 4
DEC_SEQ = 4096
PAST_LEN = 128

D_MIX = D_MODEL
GDN_HEADS = 4
GDN_DK = D_MIX // (2 * GDN_HEADS)
GDN_DV = D_MIX // (2 * GDN_HEADS)
GDN_QK = GDN_HEADS * GDN_DK
GDN_WIDTH = GDN_HEADS * GDN_DV
GDN_CONV_W = 3
GDN_CHUNK = 64
SC_WIDTH = D_MIX - GDN_WIDTH
SC_CONV_W = 3
N_MEM = 256
XATTN_HEADS = 4
XATTN_DH = D_MODEL // 8
XATTN_WIDTH = XATTN_HEADS * XATTN_DH
N_EXPERTS = 32
TOP_K = 4
D_FF = D_MODEL
SWIGLU_LIMIT = 7.0
SWIGLU_ALPHA = 1.702
MOE_BLOCK = 128
EPS = 1e-6
IN_SIZES = (GDN_QK, GDN_QK, GDN_WIDTH, GDN_WIDTH, 2 * GDN_HEADS, 2 * GDN_HEADS, SC_WIDTH, SC_WIDTH, SC_WIDTH)
D_IN = sum(IN_SIZES)

kernel_name = 'hybrid_gdn_shortconv_moe_encoder'


def _rmsnorm(x, g):
    xf = x.astype(jnp.float32)
    xf = xf * lax.rsqrt(jnp.mean(xf * xf, axis=-1, keepdims=True) + EPS)
    return xf.astype(x.dtype) * g


def _dwconv_centred(x, w):
    width = w.shape[0]
    pad = width // 2
    seq = x.shape[1]
    xp = jnp.pad(x, ((0, 0), (pad, pad), (0, 0)))
    out = xp[:, 0:seq] * w[0]
    for i in range(1, width):
        out = out + xp[:, i:i + seq] * w[i]
    return out


def _l2norm(t):
    return t * lax.rsqrt(jnp.sum(t * t, axis=-1, keepdims=True) + EPS)


def _gated_delta_rule_chunked(q, k, v, beta, g):
    b, h, seq, dk = q.shape
    dv = v.shape[-1]
    c = GDN_CHUNK
    n = seq // c
    q = (q * dk ** -0.5).reshape(b, h, n, c, dk)
    k = k.reshape(b, h, n, c, dk)
    v = v.reshape(b, h, n, c, dv)
    beta = beta.reshape(b, h, n, c)
    g = jnp.cumsum(g.reshape(b, h, n, c), axis=-1)
    incl = jnp.tril(jnp.ones((c, c), bool))
    strict = jnp.tril(jnp.ones((c, c), bool), -1)
    diff = g[..., :, None] - g[..., None, :]
    decay = jnp.where(incl, jnp.exp(jnp.where(incl, diff, 0.0)), 0.0)
    k_beta = k * beta[..., None]
    kk = jnp.where(strict, jnp.einsum('bhncd,bhnsd->bhncs', k_beta, k) * decay, 0.0)
    rhs = jnp.concatenate([v * beta[..., None], k_beta * jnp.exp(g)[..., None]], axis=-1)
    sol = lax.linalg.triangular_solve(kk, rhs, left_side=True, lower=True, unit_diagonal=True)
    u, w = sol[..., :dv], sol[..., dv:]
    qk = jnp.einsum('bhncd,bhnsd->bhncs', q, k) * decay
    q_decay = q * jnp.exp(g)[..., None]
    k_to_end = k * jnp.exp(g[..., -1:] - g)[..., None]
    chunk_decay = jnp.exp(g[..., -1])

    def step(state, inp):
        q_c, k_c, u_c, w_c, qk_c, d_c = inp
        v_new = u_c - jnp.einsum('bhcd,bhde->bhce', w_c, state)
        o = jnp.einsum('bhcd,bhde->bhce', q_c, state) + jnp.einsum('bhcs,bhse->bhce', qk_c, v_new)
        state = state * d_c[..., None, None] + jnp.einsum('bhcd,bhce->bhde', k_c, v_new)
        return state, o

    xs = tuple(jnp.moveaxis(t, 2, 0) for t in (q_decay, k_to_end, u, w, qk, chunk_decay))
    _, o = lax.scan(step, jnp.zeros((b, h, dk, dv), jnp.float32), xs)
    return jnp.moveaxis(o, 0, 2).reshape(b, h, seq, dv)


def _gdn_mixer(q, k, v, z, b_raw, a_raw, conv_w, a_log, dt_bias, norm_g):
    bsz, seq, _ = q.shape
    qkv = jax.nn.silu(_dwconv_centred(jnp.concatenate([q, k, v], axis=-1), conv_w)).astype(jnp.float32)
    q, k, v = jnp.split(qkv, [GDN_QK, 2 * GDN_QK], axis=-1)
    q = _l2norm(q.reshape(bsz, seq, GDN_HEADS, GDN_DK))
    k = _l2norm(k.reshape(bsz, seq, GDN_HEADS, GDN_DK))
    v = v.reshape(bsz, seq, GDN_HEADS, GDN_DV)
    beta = jax.nn.sigmoid(b_raw.astype(jnp.float32)).reshape(bsz, seq, 2, GDN_HEADS)
    g = -jnp.exp(a_log.astype(jnp.float32)) * jax.nn.softplus(
        a_raw.astype(jnp.float32).reshape(bsz, seq, 2, GDN_HEADS) + dt_bias.astype(jnp.float32))

    def both(t_f, t_b):
        return jnp.swapaxes(jnp.concatenate([t_f, jnp.flip(t_b, axis=1)], axis=0), 1, 2)

    o = _gated_delta_rule_chunked(both(q, q), both(k, k), both(v, v),
                                  both(beta[:, :, 0], beta[:, :, 1]), both(g[:, :, 0], g[:, :, 1]))
    o = o[:bsz] + jnp.flip(o[bsz:], axis=2)
    o = jnp.swapaxes(o, 1, 2)
    o = _rmsnorm(o, norm_g.astype(jnp.float32)) * jax.nn.silu(
        z.astype(jnp.float32).reshape(bsz, seq, GDN_HEADS, GDN_DV))
    return o.reshape(bsz, seq, GDN_WIDTH).astype(z.dtype)


def _cross_attn(h, m, w_q, w_k, w_v, w_o):
    b, seq, _ = h.shape
    n_mem = m.shape[1]
    q = (h @ w_q).reshape(b, seq, XATTN_HEADS, XATTN_DH)
    k = (m @ w_k).reshape(b, n_mem, XATTN_HEADS, XATTN_DH)
    v = (m @ w_v).reshape(b, n_mem, XATTN_HEADS, XATTN_DH)
    s = jnp.einsum('blhd,bmhd->bhlm', q, k).astype(jnp.float32) * XATTN_DH ** -0.5
    p = jax.nn.softmax(s, axis=-1).astype(v.dtype)
    o = jnp.einsum('bhlm,bmhd->blhd', p, v).reshape(b, seq, XATTN_WIDTH)
    return o @ w_o


def _clamped_swiglu(gu):
    gate = jnp.minimum(gu[..., 0::2], SWIGLU_LIMIT)
    up = jnp.clip(gu[..., 1::2], -SWIGLU_LIMIT, SWIGLU_LIMIT)
    return (up + 1.0) * gate * jax.nn.sigmoid(SWIGLU_ALPHA * gate)


def _moe(h, w_router, b_router, w_gate_up, b_gate_up, w_down, b_down):
    bsz, seq, dm = h.shape
    n_tok = bsz * seq
    xt = h.reshape(n_tok, dm)
    logits = (xt @ w_router + b_router).astype(jnp.float32)
    top_logit, top_idx = lax.top_k(logits, TOP_K)
    gates = jax.nn.softmax(top_logit, axis=-1)
    n_slot = n_tok * TOP_K
    e_flat = top_idx.reshape(n_slot)
    tok_flat = jnp.arange(n_slot, dtype=jnp.int32) // TOP_K
    order = jnp.argsort(e_flat)
    e_sorted = e_flat[order]
    counts = jnp.bincount(e_flat, length=N_EXPERTS)
    padded = (counts + MOE_BLOCK - 1) // MOE_BLOCK * MOE_BLOCK
    start = jnp.cumsum(counts) - counts
    pend = jnp.cumsum(padded)
    pstart = pend - padded
    dest = pstart[e_sorted] + jnp.arange(n_slot, dtype=jnp.int32) - start[e_sorted]
    n_blocks = (n_slot + MOE_BLOCK - 1) // MOE_BLOCK + N_EXPERTS
    n_rows = n_blocks * MOE_BLOCK
    row_tok = jnp.full((n_rows,), n_tok, jnp.int32).at[dest].set(tok_flat[order])
    row_gate = jnp.zeros((n_rows,), jnp.float32).at[dest].set(gates.reshape(n_slot)[order])
    blk_exp = jnp.minimum(jnp.searchsorted(pend, jnp.arange(n_blocks, dtype=jnp.int32) * MOE_BLOCK, side='right'),
                          N_EXPERTS - 1)
    x_pad = jnp.concatenate([xt, jnp.zeros((1, dm), xt.dtype)], axis=0)

    def block(acc, inp):
        tok, gate, e = inp
        y = _clamped_swiglu(x_pad[tok] @ w_gate_up[e] + b_gate_up[e]) @ w_down[e] + b_down[e]
        return acc.at[tok].add(y * gate[:, None].astype(y.dtype)), None

    acc, _ = lax.scan(block, jnp.zeros((n_tok + 1, dm), h.dtype),
                      (row_tok.reshape(n_blocks, MOE_BLOCK), row_gate.reshape(n_blocks, MOE_BLOCK), blk_exp))
    return acc[:n_tok].reshape(bsz, seq, dm)


def _layer(x, mem, norm_mix_g, w_in, gdn_conv_w, gdn_a_log, gdn_dt_bias, gdn_norm_g, sc_conv_w, w_out,
           norm_xattn_g, norm_mem_g, w_xq, w_xk, w_xv, w_xo,
           norm_moe_g, w_router, b_router, w_gate_up, b_gate_up, w_down, b_down):
    h = _rmsnorm(x, norm_mix_g)
    splits = [sum(IN_SIZES[:i + 1]) for i in range(len(IN_SIZES) - 1)]
    q, k, v, z, b_raw, a_raw, sc_b, sc_c, sc_h = jnp.split(h @ w_in, splits, axis=-1)
    y_gdn = _gdn_mixer(q, k, v, z, b_raw, a_raw, gdn_conv_w, gdn_a_log, gdn_dt_bias, gdn_norm_g)
    y_sc = sc_b * _dwconv_centred(sc_c * sc_h, sc_conv_w)
    x = x + jnp.concatenate([y_gdn, y_sc], axis=-1) @ w_out
    x = x + _cross_attn(_rmsnorm(x, norm_xattn_g), _rmsnorm(mem, norm_mem_g), w_xq, w_xk, w_xv, w_xo)
    x = x + _moe(_rmsnorm(x, norm_moe_g), w_router, b_router, w_gate_up, b_gate_up, w_down, b_down)
    return x


def setup_inputs(seed: int = 0) -> dict:
    key = jax.random.key(seed)
    ks = jax.random.split(key, 32)
    f32 = jnp.float32

    def dense(k, shape, fan_in):
        return jax.random.normal(k, shape, f32) * fan_in ** -0.5

    def gain(k, shape):
        return 1.0 + 0.02 * jax.random.normal(k, shape, f32)

    dt = jnp.exp(jax.random.uniform(ks[9], (DEPTH, 2, GDN_HEADS), f32, minval=math.log(1e-3), maxval=math.log(1e-1)))
    return {
        'x_prompt': jax.random.normal(ks[0], (BATCH, SEQ, D_MODEL), f32),
        'x_sample': jax.random.normal(ks[1], (DEC_BATCH, DEC_SEQ, D_MODEL), f32),
        'mem_prompt': jax.random.normal(ks[2], (BATCH, N_MEM, D_MODEL), f32),
        'mem_sample': jax.random.normal(ks[3], (DEC_BATCH, N_MEM, D_MODEL), f32),
        'norm_mix_g': gain(ks[4], (DEPTH, D_MODEL)),
        'w_in': dense(ks[5], (DEPTH, D_MODEL, D_IN), D_MODEL),
        'gdn_conv_w': dense(ks[6], (DEPTH, GDN_CONV_W, 2 * GDN_QK + GDN_WIDTH), GDN_CONV_W),
        'gdn_a_log': jnp.log(jax.random.uniform(ks[7], (DEPTH, 2, GDN_HEADS), f32, minval=1.0, maxval=16.0)),
        'gdn_dt_bias': dt + jnp.log(-jnp.expm1(-dt)),
        'gdn_norm_g': gain(ks[8], (DEPTH, GDN_DV)),
        'sc_conv_w': dense(ks[10], (DEPTH, SC_CONV_W, SC_WIDTH), SC_CONV_W),
        'w_out': dense(ks[11], (DEPTH, D_MIX, D_MODEL), D_MIX),
        'norm_xattn_g': gain(ks[12], (DEPTH, D_MODEL)),
        'norm_mem_g': gain(ks[13], (DEPTH, D_MODEL)),
        'w_xq': dense(ks[14], (DEPTH, D_MODEL, XATTN_WIDTH), D_MODEL),
        'w_xk': dense(ks[15], (DEPTH, D_MODEL, XATTN_WIDTH), D_MODEL),
        'w_xv': dense(ks[16], (DEPTH, D_MODEL, XATTN_WIDTH), D_MODEL),
        'w_xo': dense(ks[17], (DEPTH, XATTN_WIDTH, D_MODEL), XATTN_WIDTH),
        'norm_moe_g': gain(ks[18], (DEPTH, D_MODEL)),
        'w_router': dense(ks[19], (DEPTH, D_MODEL, N_EXPERTS), D_MODEL),
        'b_router': 0.01 * jax.random.normal(ks[20], (DEPTH, N_EXPERTS), f32),
        'w_gate_up': dense(ks[21], (DEPTH, N_EXPERTS, D_MODEL, 2 * D_FF), D_MODEL),
        'b_gate_up': 0.02 * jax.random.normal(ks[22], (DEPTH, N_EXPERTS, 2 * D_FF), f32),
        'w_down': dense(ks[23], (DEPTH, N_EXPERTS, D_FF, D_MODEL), D_FF),
        'b_down': 0.02 * jax.random.normal(ks[24], (DEPTH, N_EXPERTS, D_MODEL), f32),
        'norm_final_g': gain(ks[25], (D_MODEL,)),
    }


def reference(x_prompt, x_sample, mem_prompt, mem_sample, norm_mix_g, w_in, gdn_conv_w, gdn_a_log, gdn_dt_bias,
              gdn_norm_g, sc_conv_w, w_out, norm_xattn_g, norm_mem_g, w_xq, w_xk, w_xv, w_xo, norm_moe_g,
              w_router, b_router, w_gate_up, b_gate_up, w_down, b_down, norm_final_g):
    layer_params = (norm_mix_g, w_in, gdn_conv_w, gdn_a_log, gdn_dt_bias, gdn_norm_g, sc_conv_w, w_out,
                    norm_xattn_g, norm_mem_g, w_xq, w_xk, w_xv, w_xo,
                    norm_moe_g, w_router, b_router, w_gate_up, b_gate_up, w_down, b_down)

    def run(x, mem):
        for d in range(DEPTH):
            x = _layer(x, mem, *[p[d] for p in layer_params])
        return _rmsnorm(x, norm_final_g)

    y_prompt = run(x_prompt, mem_prompt)
    y_sample = run(x_sample, mem_sample)
    return (y_prompt, y_sample)
```

```python
import functools
import math

import jax
import jax.numpy as jnp
from jax import lax
from jax.experimental import pallas as pl
from jax.experimental.pallas import tpu as pltpu

F32 = jnp.float32
BF16 = jnp.bfloat16
I32 = jnp.int32

D_MODEL = 1024
GDN_HEADS = 4
HEAD_DIM = 128
GDN_WIDTH = GDN_HEADS * HEAD_DIM
SC_WIDTH = 512
CHUNK = 64
XATTN_HEADS = 4
XATTN_WIDTH = XATTN_HEADS * HEAD_DIM
N_EXPERTS = 32
TOP_K = 4
D_FF = 1024
SWIGLU_LIMIT = 7.0
SWIGLU_ALPHA = 1.702
EPS = 1e-6
LANES = 128
SUBLANES = 8
VMEM_LIMIT_BYTES = 56 * 1024 * 1024

HIGHEST = lax.Precision.HIGHEST


def _cparams(sem):
    return pltpu.CompilerParams(dimension_semantics=sem, vmem_limit_bytes=VMEM_LIMIT_BYTES)


def _rms(x, g):
    return x * lax.rsqrt(jnp.mean(x * x, axis=-1, keepdims=True) + EPS) * g


def _bdot(a, b):
    return jnp.dot(a.astype(BF16), b.astype(BF16), preferred_element_type=F32)


def _bdot_nt(a, b):
    return lax.dot_general(a.astype(BF16), b.astype(BF16), (((1,), (1,)), ((), ())),
                           preferred_element_type=F32)


def _bdot_tn(a, b):
    return lax.dot_general(a.astype(BF16), b.astype(BF16), (((0,), (0,)), ((), ())),
                           preferred_element_type=F32)


def _fdot(a, b):
    return jnp.dot(a, b, preferred_element_type=F32, precision=HIGHEST)


def _pack_bf16_pair(hi, lo):
    hi_bits = lax.bitcast_convert_type(hi.astype(BF16).astype(F32), jnp.uint32)
    lo_bits = lax.bitcast_convert_type(lo.astype(BF16).astype(F32), jnp.uint32)
    return (hi_bits & jnp.uint32(0xFFFF0000)) | (lo_bits >> 16)


def _unpack_bf16_pair(words):
    hi = lax.bitcast_convert_type(words & jnp.uint32(0xFFFF0000), F32).astype(BF16)
    lo = lax.bitcast_convert_type(words << 16, F32).astype(BF16)
    return hi, lo


def _memkv_kernel(m_ref, g_ref, wk_ref, wv_ref, k_ref, v_ref):
    h = _rms(m_ref[0], g_ref[...]).astype(BF16)
    k_ref[0] = jnp.dot(h, wk_ref[...], preferred_element_type=F32).astype(BF16)
    v_ref[0] = jnp.dot(h, wv_ref[...], preferred_element_type=F32).astype(BF16)


def _memkv(mem, g, wk, wv):
    b, n_mem, d = mem.shape
    return pl.pallas_call(
        _memkv_kernel,
        grid=(b,),
        in_specs=[pl.BlockSpec((1, n_mem, d), lambda i: (i, 0, 0)),
                  pl.BlockSpec((1, d), lambda i: (0, 0)),
                  pl.BlockSpec(wk.shape, lambda i: (0, 0)),
                  pl.BlockSpec(wv.shape, lambda i: (0, 0))],
        out_specs=[pl.BlockSpec((1, n_mem, XATTN_WIDTH), lambda i: (i, 0, 0)),
                   pl.BlockSpec((1, n_mem, XATTN_WIDTH), lambda i: (i, 0, 0))],
        out_shape=[jax.ShapeDtypeStruct((b, n_mem, XATTN_WIDTH), BF16)] * 2,
        compiler_params=_cparams(("arbitrary",)), name="memkv",
    )(mem, g, wk, wv)


def _inproj_kernel(x_ref, g_ref, wa_ref, wg_ref, wc_ref, qkv_ref, z_ref, gat_ref, scb_ref, scu_ref):
    h = _rms(x_ref[...], g_ref[...]).astype(BF16)
    nq = 3 * GDN_WIDTH
    qkv_ref[...] = jnp.dot(h, wa_ref[:, :nq], preferred_element_type=F32)
    z_ref[...] = jnp.dot(h, wa_ref[:, nq:], preferred_element_type=F32)
    gat_ref[...] = jnp.dot(h, wg_ref[...], preferred_element_type=F32)
    scb_ref[...] = jnp.dot(h, wc_ref[:, :SC_WIDTH], preferred_element_type=F32)
    sc_c = jnp.dot(h, wc_ref[:, SC_WIDTH:2 * SC_WIDTH], preferred_element_type=F32)
    sc_h = jnp.dot(h, wc_ref[:, 2 * SC_WIDTH:], preferred_element_type=F32)
    scu_ref[...] = sc_c * sc_h


def _inproj(x2d, g, wa, wg, wc, tm):
    t, d = x2d.shape
    row = lambda i: (i, 0)
    fixed = lambda i: (0, 0)
    widths = (3 * GDN_WIDTH, GDN_WIDTH, GDN_HEADS * LANES, SC_WIDTH, SC_WIDTH)
    return pl.pallas_call(
        _inproj_kernel,
        grid=(t // tm,),
        in_specs=[pl.BlockSpec((tm, d), row), pl.BlockSpec((1, d), fixed),
                  pl.BlockSpec(wa.shape, fixed), pl.BlockSpec(wg.shape, fixed), pl.BlockSpec(wc.shape, fixed)],
        out_specs=[pl.BlockSpec((tm, w), row) for w in widths],
        out_shape=[jax.ShapeDtypeStruct((t, w), F32) for w in widths],
        compiler_params=_cparams(("arbitrary",)), name="inproj",
    )(x2d, g, wa, wg, wc)


GDN_ROWS = 256


def _conv_silu(ref, r0, rows, seq, w0, w1, w2):
    x = ref[0, pl.ds(r0, rows), :]
    prev8 = ref[0, pl.ds(jnp.maximum(r0 - SUBLANES, 0), SUBLANES), :]
    next8 = ref[0, pl.ds(jnp.minimum(r0 + rows, seq - SUBLANES), SUBLANES), :]
    ridx = lax.broadcasted_iota(I32, (rows, 1), 0)
    before = jnp.where(r0 > 0, prev8[SUBLANES - 1:SUBLANES, :], 0.0)
    after = jnp.where(r0 + rows < seq, next8[0:1, :], 0.0)
    xp = jnp.where(ridx == 0, before, pltpu.roll(x, 1, axis=0))
    xn = jnp.where(ridx == rows - 1, after, pltpu.roll(x, rows - 1, axis=0))
    y = xp * w0 + x * w1 + xn * w2
    return y * jax.nn.sigmoid(y)


def _l2n(t):
    return t * lax.rsqrt(jnp.sum(t * t, axis=-1, keepdims=True) + EPS)


def _softplus(x):
    return jnp.maximum(x, 0.0) + jnp.log1p(jnp.exp(-jnp.abs(x)))


def _unit_tri_inverse(a, ri, ci):
    eye = (ri == ci).astype(F32)
    p = eye - jnp.where(((ri >> 1) == (ci >> 1)) & ((ri & 1) == 1) & ((ci & 1) == 0), a, 0.0)
    s = 2
    while s < CHUNK:
        sh = int(math.log2(2 * s))
        off = ((ri >> sh) == (ci >> sh)) & ((ri & (2 * s - 1)) >= s) & ((ci & (2 * s - 1)) < s)
        p = p - _fdot(_fdot(p, jnp.where(off, a, 0.0)), p)
        s *= 2
    return p


def _gdn_chunk(d, r0, head, alog_ref, dtb_ref, gat_ref, qn, kn, vn, s_ref, o_ref):
    q = qn[pl.ds(r0, CHUNK), :]
    k = kn[pl.ds(r0, CHUNK), :]
    v = vn[pl.ds(r0, CHUNK), :]
    gb = gat_ref[0, pl.ds(r0, CHUNK), :]
    beta = jax.nn.sigmoid(gb[:, d:d + 1])
    g = -jnp.exp(jnp.full((CHUNK, 1), alog_ref[d, head], F32)) * _softplus(gb[:, 2 + d:3 + d] + dtb_ref[d, head])
    ri = lax.broadcasted_iota(I32, (CHUNK, CHUNK), 0)
    ci = lax.broadcasted_iota(I32, (CHUNK, CHUNK), 1)
    if d == 1:
        ri, ci = ci, ri
    incl = ri >= ci
    strict = ri > ci
    gc = _fdot(incl.astype(F32), jnp.broadcast_to(g, (CHUNK, LANES)))
    eg = jnp.exp(gc)
    gct = jnp.concatenate([gc, jnp.zeros_like(gc)], axis=0).T
    decay = jnp.where(incl, jnp.exp(jnp.where(incl, gc[:, :CHUNK] - gct[:CHUNK, :CHUNK], 0.0)), 0.0)
    kb = k * beta
    kk = jnp.where(strict, _bdot_nt(kb, k) * decay, 0.0)
    p = _unit_tri_inverse(kk, ri, ci)
    u = _fdot(p, v * beta)
    w = _fdot(p, kb * eg)
    qk = _bdot_nt(q, k) * decay
    last = CHUNK - 1 if d == 0 else 0
    g_end = gc[last:last + 1, :]
    kte = k * jnp.exp(g_end - gc)
    state = s_ref[...]
    v_new = u - _bdot(w, state)
    o_ref[pl.ds(r0, CHUNK), :] = _bdot(q * eg, state) + _bdot(qk, v_new)
    s_ref[...] = state * jnp.exp(g_end) + _bdot_tn(kte, v_new)


def _gdn_kernel(alog_ref, dtb_ref, q_ref, k_ref, v_ref, z_ref, gat_ref, cw_ref, ng_ref, y_ref,
                qn, kn, vn, o_f, o_b, s_f, s_b, *, seq):
    head = pl.program_id(1)
    rows = min(GDN_ROWS, seq)

    def pre(i, _):
        r0 = pl.multiple_of(i * rows, rows)
        cw = cw_ref[0]
        tap = lambda j: cw[j:j + 1, :]
        qn[pl.ds(r0, rows), :] = _l2n(_conv_silu(q_ref, r0, rows, seq, tap(0), tap(1), tap(2))) * (HEAD_DIM ** -0.5)
        kn[pl.ds(r0, rows), :] = _l2n(_conv_silu(k_ref, r0, rows, seq, tap(3), tap(4), tap(5)))
        vn[pl.ds(r0, rows), :] = _conv_silu(v_ref, r0, rows, seq, tap(6), tap(7), tap(8))
        return 0

    lax.fori_loop(0, seq // rows, pre, 0)

    s_f[...] = jnp.zeros_like(s_f)
    s_b[...] = jnp.zeros_like(s_b)
    n_chunks = seq // CHUNK

    def step(c, _):
        _gdn_chunk(0, pl.multiple_of(c * CHUNK, CHUNK), head, alog_ref, dtb_ref, gat_ref, qn, kn, vn, s_f, o_f)
        _gdn_chunk(1, pl.multiple_of((n_chunks - 1 - c) * CHUNK, CHUNK), head, alog_ref, dtb_ref, gat_ref,
                   qn, kn, vn, s_b, o_b)
        return 0

    lax.fori_loop(0, n_chunks, step, 0)

    def post(i, _):
        r0 = pl.multiple_of(i * rows, rows)
        o = o_f[pl.ds(r0, rows), :] + o_b[pl.ds(r0, rows), :]
        z = z_ref[0, pl.ds(r0, rows), :]
        y_ref[0, pl.ds(r0, rows), :] = (_rms(o, ng_ref[...]) * (z * jax.nn.sigmoid(z))).astype(BF16)
        return 0

    lax.fori_loop(0, seq // rows, post, 0)


def _gdn(qkv, z, gat, a_log, dt_bias, cw, ng):
    b, seq, _ = qkv.shape
    col = lambda off: (lambda i, h: (i, 0, off + h))
    smem = pl.BlockSpec(memory_space=pltpu.SMEM)
    blk = (1, seq, HEAD_DIM)
    return pl.pallas_call(
        functools.partial(_gdn_kernel, seq=seq),
        grid=(b, GDN_HEADS),
        in_specs=[smem, smem,
                  pl.BlockSpec(blk, col(0)), pl.BlockSpec(blk, col(GDN_HEADS)), pl.BlockSpec(blk, col(2 * GDN_HEADS)),
                  pl.BlockSpec(blk, col(0)), pl.BlockSpec(blk, col(0)),
                  pl.BlockSpec((1, 2 * SUBLANES, HEAD_DIM), lambda i, h: (h, 0, 0)),
                  pl.BlockSpec((1, HEAD_DIM), lambda i, h: (0, 0))],
        out_specs=pl.BlockSpec(blk, col(0)),
        out_shape=jax.ShapeDtypeStruct((b, seq, GDN_WIDTH), BF16),
        scratch_shapes=[pltpu.VMEM((seq, HEAD_DIM), F32)] * 5 + [pltpu.VMEM((HEAD_DIM, HEAD_DIM), F32)] * 2,
        compiler_params=_cparams(("arbitrary", "arbitrary")), name="gdn",
    )(a_log, dt_bias, qkv, qkv, qkv, z, gat, cw, ng)


def _mid_kernel(x_ref, yg_ref, scb_ref, scu_ref, scp_ref, scn_ref, scw_ref, wo_ref, gx_ref, wq_ref,
                k_ref, v_ref, wxo_ref, gm_ref, wr_ref, br_ref, tri_ref,
                x2_ref, hm_ref, idx_ref, gate_ref, rank_ref, cnt_ref, cnt_acc, *, tm, seq):
    i = pl.program_id(0)
    tiles_per_seq = seq // tm
    pos = i % tiles_per_seq
    d_half = x_ref.shape[1] // 2

    u = scu_ref[...]
    ridx = lax.broadcasted_iota(I32, (tm, 1), 0)
    before = jnp.where(pos > 0, scp_ref[SUBLANES - 1:SUBLANES, :], 0.0)
    after = jnp.where(pos < tiles_per_seq - 1, scn_ref[0:1, :], 0.0)
    up = jnp.where(ridx == 0, before, pltpu.roll(u, 1, axis=0))
    un = jnp.where(ridx == tm - 1, after, pltpu.roll(u, tm - 1, axis=0))
    scw = scw_ref[...]
    y_sc = scb_ref[...] * (up * scw[0:1, :] + u * scw[1:2, :] + un * scw[2:3, :])

    x1 = (x_ref[...] + jnp.dot(yg_ref[...], wo_ref[:GDN_WIDTH, :], preferred_element_type=F32)
          + _bdot(y_sc, wo_ref[GDN_WIDTH:, :]))

    q = _bdot(_rms(x1, gx_ref[...]), wq_ref[...])
    kmem = k_ref[0]
    vmem = v_ref[0]
    heads = []
    for hh in range(XATTN_HEADS):
        sl = slice(hh * HEAD_DIM, (hh + 1) * HEAD_DIM)
        s = _bdot_nt(q[:, sl], kmem[:, sl]) * (HEAD_DIM ** -0.5)
        s = s - jnp.max(s, axis=-1, keepdims=True)
        e = jnp.exp(s)
        p = e / jnp.sum(e, axis=-1, keepdims=True)
        heads.append(_bdot(p, vmem[:, sl]))
    x2 = x1 + _bdot(jnp.concatenate(heads, axis=-1), wxo_ref[...])
    x2_ref[...] = x2

    hm = _rms(x2, gm_ref[...])
    hm_ref[...] = _pack_bf16_pair(hm[:, :d_half], hm[:, d_half:])
    lt = lax.dot_general(wr_ref[...], hm, (((1,), (1,)), ((), ())),
                         preferred_element_type=F32, precision=HIGHEST) + br_ref[:, 0:1]
    eidx = lax.broadcasted_iota(I32, (N_EXPERTS, tm), 0).astype(F32)
    tops, idxs = [], []
    for _ in range(TOP_K):
        m = jnp.max(lt, axis=0, keepdims=True)
        sel = jnp.min(jnp.where(lt == m, eidx, float(N_EXPERTS)), axis=0, keepdims=True)
        tops.append(m)
        idxs.append(sel)
        lt = jnp.where(eidx == sel, -jnp.inf, lt)
    es = [jnp.exp(t - tops[0]) for t in tops]
    den = es[0] + es[1] + es[2] + es[3]
    gate_ref[...] = jnp.concatenate([e / den for e in es], axis=0)
    idx_ref[...] = jnp.concatenate(idxs, axis=0).astype(I32)

    @pl.when(i == 0)
    def _():
        cnt_acc[...] = jnp.zeros_like(cnt_acc)

    onehot = jnp.zeros((N_EXPERTS, tm), F32)
    for sel in idxs:
        onehot = onehot + (eidx == sel).astype(F32)
    before_cnt = jnp.dot(onehot.astype(BF16), tri_ref[...], preferred_element_type=F32) + cnt_acc[:, 0:1]
    rank_ref[...] = jnp.concatenate(
        [jnp.sum(jnp.where(eidx == sel, before_cnt, 0.0), axis=0, keepdims=True) for sel in idxs],
        axis=0).astype(I32)
    cnt_acc[...] = cnt_acc[...] + jnp.sum(onehot, axis=1, keepdims=True)
    cnt_ref[...] = cnt_acc[...]


def _mid(x2d, yg, scb, scu, scw, wo, gx, wq, kmem, vmem, wxo, gm, wr, br, tri, tm, seq):
    t, d = x2d.shape
    n_mem = kmem.shape[1]
    tiles_per_seq = seq // tm
    halo = tm // SUBLANES
    n_halo = t // SUBLANES
    row = lambda i: (i, 0)
    fixed = lambda i: (0, 0)
    tok = lambda i: (0, i)
    return pl.pallas_call(
        functools.partial(_mid_kernel, tm=tm, seq=seq),
        grid=(t // tm,),
        in_specs=[pl.BlockSpec((tm, d), row), pl.BlockSpec((tm, GDN_WIDTH), row),
                  pl.BlockSpec((tm, SC_WIDTH), row), pl.BlockSpec((tm, SC_WIDTH), row),
                  pl.BlockSpec((SUBLANES, SC_WIDTH), lambda i: (jnp.maximum(i * halo - 1, 0), 0)),
                  pl.BlockSpec((SUBLANES, SC_WIDTH), lambda i: (jnp.minimum((i + 1) * halo, n_halo - 1), 0)),
                  pl.BlockSpec(scw.shape, fixed), pl.BlockSpec(wo.shape, fixed), pl.BlockSpec((1, d), fixed),
                  pl.BlockSpec(wq.shape, fixed),
                  pl.BlockSpec((1, n_mem, XATTN_WIDTH), lambda i: (i // tiles_per_seq, 0, 0)),
                  pl.BlockSpec((1, n_mem, XATTN_WIDTH), lambda i: (i // tiles_per_seq, 0, 0)),
                  pl.BlockSpec(wxo.shape, fixed), pl.BlockSpec((1, d), fixed),
                  pl.BlockSpec(wr.shape, fixed), pl.BlockSpec(br.shape, fixed), pl.BlockSpec(tri.shape, fixed)],
        out_specs=[pl.BlockSpec((tm, d), row), pl.BlockSpec((tm, d // 2), row),
                   pl.BlockSpec((TOP_K, tm), tok), pl.BlockSpec((TOP_K, tm), tok), pl.BlockSpec((TOP_K, tm), tok),
                   pl.BlockSpec((N_EXPERTS, LANES), fixed)],
        out_shape=[jax.ShapeDtypeStruct((t, d), F32), jax.ShapeDtypeStruct((t, d // 2), jnp.uint32),
                   jax.ShapeDtypeStruct((TOP_K, t), I32), jax.ShapeDtypeStruct((TOP_K, t), F32),
                   jax.ShapeDtypeStruct((TOP_K, t), I32), jax.ShapeDtypeStruct((N_EXPERTS, LANES), F32)],
        scratch_shapes=[pltpu.VMEM((N_EXPERTS, LANES), F32)],
        compiler_params=_cparams(("arbitrary",)), name="mid",
    )(x2d, yg, scb, scu, scu, scu, scw, wo, gx, wq, kmem, vmem, wxo, gm, wr, br, tri)


def _row_copy(src_ref, src_row, dst_ref, dst_row, sem):
    return pltpu.make_async_copy(src_ref.at[pl.ds(src_row, 1)], dst_ref.at[pl.ds(dst_row, 1)], sem)


def _dispatch_kernel(idx_ref, rank_ref, pstart_ref, hm_ref, xs_in_ref, xs_ref, sem, *, tm):
    del xs_in_ref
    base = pl.program_id(0) * tm

    def issue(t, _):
        for kk in range(TOP_K):
            dest = pstart_ref[idx_ref[kk, t]] + rank_ref[kk, t]
            _row_copy(hm_ref, base + t, xs_ref, dest, sem).start()
        return 0

    lax.fori_loop(0, tm, issue, 0)

    def drain(t, _):
        for kk in range(TOP_K):
            _row_copy(hm_ref, 0, xs_ref, 0, sem).wait()
        return 0

    lax.fori_loop(0, tm, drain, 0)


def _dispatch(idx, rank, pstart, hm, xs0, tm):
    t = hm.shape[0]
    tok = lambda i: (0, i)
    return pl.pallas_call(
        functools.partial(_dispatch_kernel, tm=tm),
        grid=(t // tm,),
        in_specs=[pl.BlockSpec((TOP_K, tm), tok, memory_space=pltpu.SMEM),
                  pl.BlockSpec((TOP_K, tm), tok, memory_space=pltpu.SMEM),
                  pl.BlockSpec(memory_space=pltpu.SMEM),
                  pl.BlockSpec(memory_space=pl.ANY), pl.BlockSpec(memory_space=pl.ANY)],
        out_specs=pl.BlockSpec(memory_space=pl.ANY),
        out_shape=jax.ShapeDtypeStruct(xs0.shape, xs0.dtype),
        scratch_shapes=[pltpu.SemaphoreType.DMA],
        input_output_aliases={4: 0},
        compiler_params=_cparams(("arbitrary",)), name="dispatch",
    )(idx, rank, pstart, hm, xs0)


def _expert_kernel(blk_exp_ref, n_used_ref, xs_ref, wg_ref, wu_ref, bg_ref, bu_ref, wd_ref, bd_ref, ys_ref):
    del blk_exp_ref
    i = pl.program_id(0)

    @pl.when(i < n_used_ref[0])
    def _():
        x_hi, x_lo = _unpack_bf16_pair(xs_ref[...])
        half = x_hi.shape[1]

        def proj(w_ref):
            return (jnp.dot(x_hi, w_ref[0, :half, :], preferred_element_type=F32)
                    + jnp.dot(x_lo, w_ref[0, half:, :], preferred_element_type=F32))

        gate = jnp.minimum(proj(wg_ref) + bg_ref[0], SWIGLU_LIMIT)
        up = jnp.clip(proj(wu_ref) + bu_ref[0], -SWIGLU_LIMIT, SWIGLU_LIMIT)
        act = (up + 1.0) * gate * jax.nn.sigmoid(SWIGLU_ALPHA * gate)
        ys_ref[...] = _bdot(act, wd_ref[0]) + bd_ref[0]

    @pl.when(i >= n_used_ref[0])
    def _():
        ys_ref[...] = jnp.zeros_like(ys_ref)


def _experts(blk_exp, n_used, xs, wg, wu, bg, bu, wd, bd, bm):
    n_rows = xs.shape[0]
    d = wg.shape[1]
    n_blocks = n_rows // bm
    ew = lambda i, be, nu: (be[i], 0, 0)
    row = lambda i, be, nu: (i, 0)
    return pl.pallas_call(
        _expert_kernel,
        grid_spec=pltpu.PrefetchScalarGridSpec(
            num_scalar_prefetch=2,
            grid=(n_blocks,),
            in_specs=[pl.BlockSpec((bm, d // 2), row),
                      pl.BlockSpec((1, d, D_FF), ew), pl.BlockSpec((1, d, D_FF), ew),
                      pl.BlockSpec((1, 1, D_FF), ew), pl.BlockSpec((1, 1, D_FF), ew),
                      pl.BlockSpec((1, D_FF, d), ew), pl.BlockSpec((1, 1, d), ew)],
            out_specs=pl.BlockSpec((bm, d), row)),
        out_shape=jax.ShapeDtypeStruct((n_rows, d), F32),
        compiler_params=_cparams(("arbitrary",)), name="experts",
    )(blk_exp, n_used, xs, wg, wu, bg, bu, wd, bd)


def _combine_kernel(idx_ref, rank_ref, pstart_ref, gate_ref, x2_ref, gf_ref, ys_ref, y_ref, buf, sem, *, tm):
    def issue(t, _):
        for kk in range(TOP_K):
            src = pstart_ref[idx_ref[kk, t]] + rank_ref[kk, t]
            _row_copy(ys_ref, src, buf.at[kk], t, sem).start()
        return 0

    lax.fori_loop(0, tm, issue, 0)

    def drain(t, _):
        for kk in range(TOP_K):
            _row_copy(ys_ref, 0, buf.at[kk], 0, sem).wait()
        return 0

    lax.fori_loop(0, tm, drain, 0)

    gcol = jnp.concatenate([gate_ref[...], jnp.zeros((LANES - TOP_K, tm), F32)], axis=0).T
    acc = x2_ref[...]
    for kk in range(TOP_K):
        acc = acc + buf[kk] * gcol[:, kk:kk + 1]
    y_ref[...] = _rms(acc, gf_ref[...])


def _combine(idx, rank, pstart, gates, x2, gf, ys, tm):
    t, d = x2.shape
    tok = lambda i: (0, i)
    return pl.pallas_call(
        functools.partial(_combine_kernel, tm=tm),
        grid=(t // tm,),
        in_specs=[pl.BlockSpec((TOP_K, tm), tok, memory_space=pltpu.SMEM),
                  pl.BlockSpec((TOP_K, tm), tok, memory_space=pltpu.SMEM),
                  pl.BlockSpec(memory_space=pltpu.SMEM),
                  pl.BlockSpec((TOP_K, tm), tok),
                  pl.BlockSpec((tm, d), lambda i: (i, 0)),
                  pl.BlockSpec((1, d), lambda i: (0, 0)),
                  pl.BlockSpec(memory_space=pl.ANY)],
        out_specs=pl.BlockSpec((tm, d), lambda i: (i, 0)),
        out_shape=jax.ShapeDtypeStruct((t, d), F32),
        scratch_shapes=[pltpu.VMEM((TOP_K, tm, d), F32), pltpu.SemaphoreType.DMA],
        compiler_params=_cparams(("arbitrary",)), name="combine",
    )(idx, rank, pstart, gates, x2, gf, ys)


def _tile(n, pref):
    t = pref
    while n % t:
        t //= 2
    return t


def _prep_weights(norm_mix_g, w_in, gdn_conv_w, gdn_a_log, gdn_dt_bias, gdn_norm_g, sc_conv_w, w_out,
                  norm_xattn_g, norm_mem_g, w_xq, w_xk, w_xv, w_xo, norm_moe_g, w_router, b_router,
                  w_gate_up, b_gate_up, w_down, b_down, norm_final_g):
    w_in = w_in[0]
    nqkvz = 4 * GDN_WIDTH
    n_gate = 2 * GDN_HEADS
    wa = w_in[:, :nqkvz].astype(BF16)
    wb = w_in[:, nqkvz:nqkvz + n_gate]
    wdec = w_in[:, nqkvz + n_gate:nqkvz + 2 * n_gate]
    groups = []
    for h in range(GDN_HEADS):
        cols = jnp.stack([wb[:, h], wb[:, GDN_HEADS + h], wdec[:, h], wdec[:, GDN_HEADS + h]], axis=1)
        groups.append(jnp.pad(cols, ((0, 0), (0, LANES - 4))))
    wg = jnp.concatenate(groups, axis=1).astype(BF16)
    wc = w_in[:, nqkvz + 2 * n_gate:].astype(BF16)
    cw = gdn_conv_w[0]
    cw = cw.reshape(3, 3, GDN_HEADS, HEAD_DIM)
    cw = jnp.transpose(cw, (2, 1, 0, 3)).reshape(GDN_HEADS, 9, HEAD_DIM)
    cw = jnp.pad(cw, ((0, 0), (0, 2 * SUBLANES - 9), (0, 0)))
    scw = jnp.pad(sc_conv_w[0], ((0, SUBLANES - 3), (0, 0)))
    wgu = w_gate_up[0]
    return dict(
        g_mix=norm_mix_g, wa=wa, wg=wg, wc=wc, cw=cw, a_log=gdn_a_log[0], dt_bias=gdn_dt_bias[0],
        ng=gdn_norm_g, scw=scw, wo=w_out[0].astype(BF16), gx=norm_xattn_g, gmem=norm_mem_g,
        wq=w_xq[0].astype(BF16), wk=w_xk[0].astype(BF16), wv=w_xv[0].astype(BF16), wxo=w_xo[0].astype(BF16),
        gm=norm_moe_g, wr=jnp.transpose(w_router[0]),
        br=jnp.broadcast_to(b_router[0][:, None], (N_EXPERTS, LANES)),
        w_gate=wgu[:, :, 0::2].astype(BF16), w_up=wgu[:, :, 1::2].astype(BF16),
        b_gate=b_gate_up[0][:, None, 0::2], b_up=b_gate_up[0][:, None, 1::2],
        w_down=w_down[0].astype(BF16), b_down=b_down[0][:, None, :],
        gf=norm_final_g[None, :],
    )


def _run_group(x, mem, p, *, tm=512, bm=256, tm_moe=256):
    b, seq, d = x.shape
    t = b * seq
    x2d = x.reshape(t, d)
    tm = _tile(seq, tm)
    tm_moe = _tile(t, tm_moe)

    kmem, vmem = _memkv(mem, p["gmem"], p["wk"], p["wv"])
    qkv, z, gat, scb, scu = _inproj(x2d, p["g_mix"], p["wa"], p["wg"], p["wc"], tm)
    yg = _gdn(qkv.reshape(b, seq, -1), z.reshape(b, seq, -1), gat.reshape(b, seq, -1),
              p["a_log"], p["dt_bias"], p["cw"], p["ng"])
    tri = (lax.broadcasted_iota(I32, (tm, tm), 0) < lax.broadcasted_iota(I32, (tm, tm), 1)).astype(BF16)
    x2, hm, idx, gates, rank, cnt = _mid(
        x2d, yg.reshape(t, GDN_WIDTH), scb, scu, p["scw"], p["wo"], p["gx"], p["wq"], kmem, vmem, p["wxo"],
        p["gm"], p["wr"], p["br"], tri, tm, seq)

    counts = cnt[:, 0].astype(I32)
    padded = (counts + bm - 1) // bm * bm
    pend = jnp.cumsum(padded)
    pstart = (pend - padded).astype(I32)
    n_blocks = (t * TOP_K) // bm + N_EXPERTS
    blk_exp = jnp.minimum(
        jnp.searchsorted(pend, jnp.arange(n_blocks, dtype=I32) * bm, side="right"), N_EXPERTS - 1).astype(I32)
    n_used = (pend[-1:] // bm).astype(I32)

    xs = _dispatch(idx, rank, pstart, hm, jnp.zeros((n_blocks * bm, d // 2), jnp.uint32), tm_moe)
    ys = _experts(blk_exp, n_used, xs, p["w_gate"], p["w_up"], p["b_gate"], p["b_up"], p["w_down"], p["b_down"], bm)
    y = _combine(idx, rank, pstart, gates, x2, p["gf"], ys, tm_moe)
    return y.reshape(b, seq, d)


def kernel(x_prompt, x_sample, mem_prompt, mem_sample, norm_mix_g, w_in, gdn_conv_w, gdn_a_log, gdn_dt_bias,
           gdn_norm_g, sc_conv_w, w_out, norm_xattn_g, norm_mem_g, w_xq, w_xk, w_xv, w_xo, norm_moe_g,
           w_router, b_router, w_gate_up, b_gate_up, w_down, b_down, norm_final_g):
    p = _prep_weights(norm_mix_g, w_in, gdn_conv_w, gdn_a_log, gdn_dt_bias, gdn_norm_g, sc_conv_w, w_out,
                      norm_xattn_g, norm_mem_g, w_xq, w_xk, w_xv, w_xo, norm_moe_g, w_router, b_router,
                      w_gate_up, b_gate_up, w_down, b_down, norm_final_g)
    return (_run_group(x_prompt, mem_prompt, p), _run_group(x_sample, mem_sample, p))
```
